```python
import math
import jax
import jax.numpy as jnp
from jax import lax
import numpy as np

D_MODEL = 1024
BATCH = 8
SEQ = 2048
DEPTH = 2
DEC_BATCH = 32
DEC_SEQ = 1
PAST_LEN = 16384
PAGE_SIZE = 128

N_HEADS = 4
HEAD_DIM = 64
V_DIM = 2 * HEAD_DIM
ATTN_WIDTH = N_HEADS * V_DIM
QK_COLS = N_HEADS * 2 * HEAD_DIM
CONV_CH = D_MODEL - ATTN_WIDTH
IN_COLS = 2 * QK_COLS + ATTN_WIDTH + 2 * CONV_CH
CONV_WIDTH = 31
D_PLE = 256
DENSE_FF = 2816
N_EXPERTS = 8
TOP_K = 2
EXPERT_FF = 1408
Q_BLOCK = 128
ATTN_SCALE = HEAD_DIM ** -0.5
RMS_EPS = 1e-6
LN_EPS = 1e-5

kernel_name = 'hymba_diffattn_conformer_moe_step'


def rms_norm(x, g):
    xf = x.astype(jnp.float32)
    y = xf * lax.rsqrt(jnp.mean(xf * xf, axis=-1, keepdims=True) + RMS_EPS)
    return (y * g.astype(jnp.float32)).astype(x.dtype)


def layer_norm(x, g, b):
    xf = x.astype(jnp.float32)
    mu = jnp.mean(xf, axis=-1, keepdims=True)
    xc = xf - mu
    var = jnp.mean(xc * xc, axis=-1, keepdims=True)
    y = xc * lax.rsqrt(var + LN_EPS) * g.astype(jnp.float32) + b.astype(jnp.float32)
    return y.astype(x.dtype)


def lambda_init(i):
    return 0.8 - 0.6 * math.exp(-0.3 * i)


def diff_lambda(lq, lam_init):
    lq = lq.astype(jnp.float32)
    return jnp.exp(jnp.sum(lq[0] * lq[1])) - jnp.exp(jnp.sum(lq[2] * lq[3])) + lam_init


def diff_weights(s, lam):
    p = jax.nn.softmax(s, axis=-1)
    return p[:, :, 0] - lam * p[:, :, 1]


def prompt_diff_attention(q, k, v, lam):
    b, t = q.shape[0], q.shape[1]
    nb = t // Q_BLOCK
    qb = jnp.moveaxis(q.reshape(b, nb, Q_BLOCK, N_HEADS, 2, HEAD_DIM), 1, 0)
    k_pos = jnp.arange(t)

    def one_block(args):
        q_blk, start = args
        s = jnp.einsum('bqhcd,bkhcd->bhcqk', q_blk, k).astype(jnp.float32) * ATTN_SCALE
        q_pos = start + jnp.arange(Q_BLOCK)
        s = jnp.where(k_pos[None, :] <= q_pos[:, None], s, -jnp.inf)
        a = diff_weights(s, lam).astype(v.dtype)
        return jnp.einsum('bhqk,bkhv->bqhv', a, v)

    o = lax.map(one_block, (qb, jnp.arange(nb) * Q_BLOCK))
    return jnp.moveaxis(o, 0, 1).reshape(b, t, N_HEADS, V_DIM)


def sample_diff_attention(q, k_new, v_new, cache_k, cache_v, layer, page_table, lam):
    b, s_len = q.shape[0], q.shape[1]
    k_past = cache_k[layer, page_table].reshape(b, -1, N_HEADS, 2, HEAD_DIM)
    v_past = cache_v[layer, page_table].reshape(b, -1, N_HEADS, V_DIM)
    past = k_past.shape[1]
    s_past = jnp.einsum('bqhcd,bkhcd->bhcqk', q, k_past).astype(jnp.float32) * ATTN_SCALE
    s_new = jnp.einsum('bqhcd,bkhcd->bhcqk', q, k_new).astype(jnp.float32) * ATTN_SCALE
    causal = jnp.tril(jnp.ones((s_len, s_len), dtype=bool))
    s_new = jnp.where(causal, s_new, -jnp.inf)
    a = diff_weights(jnp.concatenate([s_past, s_new], axis=-1), lam).astype(v_new.dtype)
    return (jnp.einsum('bhqk,bkhv->bqhv', a[..., :past], v_past)
            + jnp.einsum('bhqk,bkhv->bqhv', a[..., past:], v_new))


def depthwise_causal_conv(buf, w, b):
    rhs = w[:, None, :].astype(buf.dtype)
    y = lax.conv_general_dilated(buf, rhs, window_strides=(1,), padding='VALID',
                                 dimension_numbers=('NWC', 'WIO', 'NWC'),
                                 feature_group_count=buf.shape[-1])
    return y + b


def swiglu(x, w_gu, w_down):
    g, u = jnp.split(x @ w_gu, 2, axis=-1)
    return (jax.nn.silu(g) * u) @ w_down


def moe_swiglu(x, w_router, b_router, w_gu, w_down):
    logits = (x @ w_router + b_router).astype(jnp.float32)
    top_v, top_i = lax.top_k(logits, TOP_K)
    top_w = jax.nn.softmax(top_v, axis=-1)
    gates = jnp.sum(jax.nn.one_hot(top_i, N_EXPERTS, dtype=jnp.float32) * top_w[..., None],
                    axis=-2).astype(x.dtype)
    y = jnp.zeros_like(x)
    for e in range(N_EXPERTS):
        y = y + gates[..., e:e + 1] * swiglu(x, w_gu[e], w_down[e])
    return y


def setup_inputs(seed: int = 0) -> dict:
    key = jax.random.key(seed)
    ks = iter(jax.random.split(key, 40))
    f32 = jnp.float32

    def nrm(shape, scale):
        return jax.random.normal(next(ks), shape, f32) * scale

    def gain(shape):
        return 1.0 + 0.02 * jax.random.normal(next(ks), shape, f32)

    n_pages = PAST_LEN // PAGE_SIZE
    n_pool = (5 * DEC_BATCH * n_pages + 3) // 4
    n_dense = (DEPTH + 1) // 2
    n_moe = DEPTH // 2
    perm = jax.random.permutation(next(ks), n_pool)
    page_table = perm[:DEC_BATCH * n_pages].reshape(DEC_BATCH, n_pages).astype(jnp.int32)
    return {
        'x_prompt': nrm((BATCH, SEQ, D_MODEL), 1.0),
        'x_sample': nrm((DEC_BATCH, DEC_SEQ, D_MODEL), 1.0),
        'cache_k': nrm((DEPTH, n_pool, PAGE_SIZE, N_HEADS, 2, HEAD_DIM), 1.0),
        'cache_v': nrm((DEPTH, n_pool, PAGE_SIZE, N_HEADS, V_DIM), 1.0),
        'state_conv': nrm((DEPTH, DEC_BATCH, CONV_WIDTH - 1, CONV_CH), 0.5),
        'page_table': page_table,
        'p_prompt': nrm((DEPTH, BATCH, SEQ, D_PLE), 1.0),
        'p_sample': nrm((DEPTH, DEC_BATCH, DEC_SEQ, D_PLE), 1.0),
        'g_mix': gain((DEPTH, D_MODEL)),
        'w_in': nrm((DEPTH, D_MODEL, IN_COLS), D_MODEL ** -0.5),
        'q_norm': gain((DEPTH, HEAD_DIM)),
        'k_norm': gain((DEPTH, HEAD_DIM)),
        'lambda_qk': nrm((DEPTH, 4, HEAD_DIM), 0.1),
        'g_subln': gain((DEPTH, V_DIM)),
        'conv_w': nrm((DEPTH, CONV_WIDTH, CONV_CH), CONV_WIDTH ** -0.5),
        'conv_b': nrm((DEPTH, CONV_CH), 0.02),
        'conv_ln_g': gain((DEPTH, CONV_CH)),
        'conv_ln_b': nrm((DEPTH, CONV_CH), 0.02),
        'w_out': nrm((DEPTH, D_MODEL, D_MODEL), D_MODEL ** -0.5),
        'g_ffn': gain((DEPTH, D_MODEL)),
        'w_dense_gu': nrm((n_dense, D_MODEL, 2 * DENSE_FF), D_MODEL ** -0.5),
        'w_dense_down': nrm((n_dense, DENSE_FF, D_MODEL), DENSE_FF ** -0.5),
        'w_router': nrm((n_moe, D_MODEL, N_EXPERTS), D_MODEL ** -0.5),
        'b_router': nrm((n_moe, N_EXPERTS), 0.01),
        'w_exp_gu': nrm((n_moe, N_EXPERTS, D_MODEL, 2 * EXPERT_FF), D_MODEL ** -0.5),
        'w_exp_down': nrm((n_moe, N_EXPERTS, EXPERT_FF, D_MODEL), EXPERT_FF ** -0.5),
        'g_ple_gate': gain((DEPTH, D_MODEL)),
        'w_ple_gate': nrm((DEPTH, D_MODEL, D_MODEL), D_MODEL ** -0.5),
        'w_ple': nrm((DEPTH, D_PLE, D_MODEL), D_PLE ** -0.5),
        'g_ple_out': gain((DEPTH, D_MODEL)),
    }


def reference(x_prompt, x_sample, cache_k, cache_v, state_conv, page_table, p_prompt, p_sample,
              g_mix, w_in, q_norm, k_norm, lambda_qk, g_subln, conv_w, conv_b, conv_ln_g,
              conv_ln_b, w_out, g_ffn, w_dense_gu, w_dense_down, w_router, b_router,
              w_exp_gu, w_exp_down, g_ple_gate, w_ple_gate, w_ple, g_ple_out):
    pad = CONV_WIDTH - 1

    def pre(h, i):
        a = rms_norm(h, g_mix[i])
        z = a @ w_in[i]
        lead = z.shape[:-1]
        q = z[..., :QK_COLS].reshape(lead + (N_HEADS, 2, HEAD_DIM))
        k = z[..., QK_COLS:2 * QK_COLS].reshape(lead + (N_HEADS, 2, HEAD_DIM))
        v = z[..., 2 * QK_COLS:2 * QK_COLS + ATTN_WIDTH].reshape(lead + (N_HEADS, V_DIM))
        u = z[..., 2 * QK_COLS + ATTN_WIDTH:]
        q = rms_norm(q, q_norm[i])
        k = rms_norm(k, k_norm[i])
        glu = u[..., :CONV_CH] * jax.nn.sigmoid(u[..., CONV_CH:])
        return q, k, v, glu

    def post(h, o, buf, p_i, i):
        o = rms_norm(o, g_subln[i]) * (1.0 - lambda_init(i))
        o = o.reshape(o.shape[:-2] + (ATTN_WIDTH,))
        c = depthwise_causal_conv(buf, conv_w[i], conv_b[i])
        c = jax.nn.silu(layer_norm(c, conv_ln_g[i], conv_ln_b[i]))
        h = h + jnp.concatenate([o, c], axis=-1) @ w_out[i]
        b = rms_norm(h, g_ffn[i])
        if i % 2 == 0:
            h = h + swiglu(b, w_dense_gu[i // 2], w_dense_down[i // 2])
        else:
            j = i // 2
            h = h + moe_swiglu(b, w_router[j], b_router[j], w_exp_gu[j], w_exp_down[j])
        gate = jax.nn.sigmoid(rms_norm(h, g_ple_gate[i]) @ w_ple_gate[i])
        return h + rms_norm(p_i @ w_ple[i], g_ple_out[i]) * gate

    hp, hs = x_prompt, x_sample
    nk_p, nv_p, nc_p, nk_s, nv_s, nc_s = [], [], [], [], [], []
    for i in range(DEPTH):
        lam = diff_lambda(lambda_qk[i], lambda_init(i))
        q, k, v, glu = pre(hp, i)
        o = prompt_diff_attention(q, k, v, lam)
        buf = jnp.concatenate([jnp.zeros((glu.shape[0], pad, CONV_CH), glu.dtype), glu], axis=1)
        hp = post(hp, o, buf, p_prompt[i], i)
        nk_p.append(k)
        nv_p.append(v)
        nc_p.append(buf[:, -pad:])
        q, k, v, glu = pre(hs, i)
        o = sample_diff_attention(q, k, v, cache_k, cache_v, i, page_table, lam)
        buf = jnp.concatenate([state_conv[i].astype(glu.dtype), glu], axis=1)
        hs = post(hs, o, buf, p_sample[i], i)
        nk_s.append(k)
        nv_s.append(v)
        nc_s.append(buf[:, -pad:])
    return (hp, hs, jnp.stack(nk_p), jnp.stack(nv_p), jnp.stack(nc_p),
            jnp.stack(nk_s), jnp.stack(nv_s), jnp.stack(nc_s))
```

```python
import functools
import math

import jax
import jax.numpy as jnp
from jax import lax
from jax.experimental import pallas as pl
from jax.experimental.pallas import tpu as pltpu

D_MODEL = 1024
N_HEADS = 4
HEAD_DIM = 64
V_DIM = 2 * HEAD_DIM
ATTN_WIDTH = N_HEADS * V_DIM
QK_COLS = N_HEADS * 2 * HEAD_DIM
CONV_CH = D_MODEL - ATTN_WIDTH
IN_COLS = 2 * QK_COLS + ATTN_WIDTH + 2 * CONV_CH
CONV_WIDTH = 31
CONV_PAD = CONV_WIDTH - 1
D_PLE = 256
DENSE_FF = 2816
N_EXPERTS = 8
EXPERT_FF = 1408
PAGE_SIZE = 128
ATTN_SCALE = HEAD_DIM ** -0.5
RMS_EPS = 1e-6
LN_EPS = 1e-5

LANES = 128
SUBLANES = 8
HALO_ROWS = 32
VMEM_LIMIT = 56 * 1024 * 1024

BF16 = jnp.bfloat16
F32 = jnp.float32


def _lambda_init(i):
    return 0.8 - 0.6 * math.exp(-0.3 * i)


def _params(*sem):
    return pltpu.CompilerParams(dimension_semantics=sem, vmem_limit_bytes=VMEM_LIMIT)


def _resident(shape):
    nd = len(shape)
    return pl.BlockSpec(shape, lambda *_: (0,) * nd, pipeline_mode=pl.Buffered(1))


def _rms(x, g):
    return x * lax.rsqrt(jnp.mean(x * x, axis=-1, keepdims=True) + RMS_EPS) * g


def _dot(a, b):
    return jnp.dot(a, b, preferred_element_type=F32)


def _dot_nt(a, b):
    return lax.dot_general(a, b, (((1,), (1,)), ((), ())), preferred_element_type=F32)


def _silu(x):
    return x * jax.nn.sigmoid(x)


def _diff_lambda(lq, lam_init):
    a = jnp.sum(lq[0:1] * lq[1:2], axis=-1, keepdims=True)
    b = jnp.sum(lq[2:3] * lq[3:4], axis=-1, keepdims=True)
    return jnp.exp(a) - jnp.exp(b) + lam_init


def _inproj_kernel(h_ref, g_ref, w_ref, qn_ref, kn_ref, seg_ref,
                   q_ref, k_ref, v_ref, glu_ref):
    a = _rms(h_ref[...], g_ref[...]).astype(BF16)
    seg = seg_ref[...]

    def head_norm(z, gain):
        msq = _dot((z * z).astype(BF16), seg) * (1.0 / HEAD_DIM)
        return z * lax.rsqrt(msq + RMS_EPS) * gain

    q_ref[...] = head_norm(_dot(a, w_ref[:, 0:QK_COLS]), qn_ref[...])
    k_ref[...] = head_norm(_dot(a, w_ref[:, QK_COLS:2 * QK_COLS]), kn_ref[...])
    c0 = 2 * QK_COLS
    v_ref[...] = _dot(a, w_ref[:, c0:c0 + ATTN_WIDTH])
    c1 = c0 + ATTN_WIDTH
    u = _dot(a, w_ref[:, c1:c1 + CONV_CH])
    gate = _dot(a, w_ref[:, c1 + CONV_CH:c1 + 2 * CONV_CH])
    glu_ref[...] = u * jax.nn.sigmoid(gate)


def _inproj(h, g, w_b, qn, kn, seg, tm):
    m = h.shape[0]
    row = lambda i: (i, 0)
    out = jax.ShapeDtypeStruct((m, QK_COLS), F32)
    return pl.pallas_call(
        _inproj_kernel,
        grid=(m // tm,),
        in_specs=[pl.BlockSpec((tm, D_MODEL), row), _resident((1, D_MODEL)),
                  _resident((D_MODEL, IN_COLS)), _resident((1, QK_COLS)),
                  _resident((1, QK_COLS)), _resident((QK_COLS, QK_COLS))],
        out_specs=[pl.BlockSpec((tm, QK_COLS), row)] * 4,
        out_shape=[out] * 4,
        compiler_params=_params("parallel"),
        name="inproj",
    )(h, g, w_b, qn, kn, seg)


def _attn_prompt_kernel(lq_ref, gs_ref, q_ref, k_ref, v_ref, o_ref,
                        kb_sc, vb_sc, m_sc, l_sc, acc_sc, *, lam_init, seq, tq):
    lam = _diff_lambda(lq_ref[...], lam_init)
    kb_sc[...] = k_ref[...].astype(BF16)
    vb_sc[...] = v_ref[...].astype(BF16)
    lane = lax.broadcasted_iota(jnp.int32, (tq, V_DIM), 1)
    gs = gs_ref[...]

    def kv_step(q12, j, masked):
        k0 = pl.multiple_of(j * tq, tq)
        s = _dot_nt(q12, kb_sc[pl.ds(k0, tq), :])
        if masked:
            r = lax.broadcasted_iota(jnp.int32, (2 * tq, tq), 0)
            c = lax.broadcasted_iota(jnp.int32, (2 * tq, tq), 1)
            r = jnp.where(r >= tq, r - tq, r)
            s = jnp.where(c <= r, s, -jnp.inf)
        m_prev = m_sc[...]
        m_new = jnp.maximum(m_prev, jnp.max(s, axis=-1, keepdims=True))
        alpha = jnp.exp(m_prev - m_new)
        p = jnp.exp(s - m_new)
        l_sc[...] = alpha * l_sc[...] + jnp.sum(p, axis=-1, keepdims=True)
        acc_sc[...] = alpha * acc_sc[...] + _dot(p.astype(BF16), vb_sc[pl.ds(k0, tq), :])
        m_sc[...] = m_new

    def q_body(qi, carry):
        q0 = pl.multiple_of(qi * tq, tq)
        qblk = q_ref[pl.ds(q0, tq), :] * ATTN_SCALE
        q1 = jnp.where(lane < HEAD_DIM, qblk, 0.0).astype(BF16)
        q2 = jnp.where(lane >= HEAD_DIM, qblk, 0.0).astype(BF16)
        q12 = jnp.concatenate([q1, q2], axis=0)
        m_sc[...] = jnp.full(m_sc.shape, -jnp.inf, F32)
        l_sc[...] = jnp.zeros(l_sc.shape, F32)
        acc_sc[...] = jnp.zeros(acc_sc.shape, F32)

        def off_diag(j, c):
            kv_step(q12, j, False)
            return c

        lax.fori_loop(0, qi, off_diag, 0)
        kv_step(q12, qi, True)
        on = acc_sc[...] / l_sc[...]
        o = on[:tq] - lam * on[tq:]
        o_ref[pl.ds(q0, tq), :] = _rms(o, gs) * (1.0 - lam_init)
        return carry

    lax.fori_loop(0, seq // tq, q_body, 0)


def _attn_prompt(q, k, v, lq, gs, lam_init, batch, seq, tq=256):
    blk = pl.BlockSpec((seq, V_DIM), lambda b, h: (b, h))
    return pl.pallas_call(
        functools.partial(_attn_prompt_kernel, lam_init=lam_init, seq=seq, tq=tq),
        grid=(batch, N_HEADS),
        in_specs=[_resident((4, HEAD_DIM)), _resident((1, V_DIM)), blk, blk, blk],
        out_specs=blk,
        out_shape=jax.ShapeDtypeStruct((batch * seq, ATTN_WIDTH), F32),
        scratch_shapes=[pltpu.VMEM((seq, V_DIM), BF16), pltpu.VMEM((seq, V_DIM), BF16),
                        pltpu.VMEM((2 * tq, 1), F32), pltpu.VMEM((2 * tq, 1), F32),
                        pltpu.VMEM((2 * tq, V_DIM), F32)],
        compiler_params=_params("parallel", "parallel"),
        name="attn_prompt",
    )(lq, gs, q, k, v)


def _attn_decode_kernel(pt_ref, lq_ref, gs_ref, q_ref, kn_ref, vn_ref, *rest,
                        lam_init, group):
    del pt_ref
    k_pages = rest[:group]
    v_pages = rest[group:2 * group]
    o_ref, m_sc, l_sc, acc_sc = rest[2 * group:]
    b = pl.program_id(0)
    j = pl.program_id(1)
    rows = 2 * N_HEADS
    r = lax.broadcasted_iota(jnp.int32, (rows, QK_COLS), 0)
    c = lax.broadcasted_iota(jnp.int32, (rows, QK_COLS), 1)
    qrow = q_ref[pl.ds(b, 1), :] * ATTN_SCALE
    qbd = jnp.where(c // HEAD_DIM == r, qrow, 0.0)

    @pl.when(j == 0)
    def _():
        m_sc[...] = jnp.full(m_sc.shape, -jnp.inf, F32)
        l_sc[...] = jnp.zeros(l_sc.shape, F32)
        acc_sc[...] = jnp.zeros(acc_sc.shape, F32)

    qb = qbd.astype(BF16)
    s = jnp.concatenate([_dot(qb, kp[...].astype(BF16)) for kp in k_pages], axis=1)
    m_prev = m_sc[...]
    m_new = jnp.maximum(m_prev, jnp.max(s, axis=-1, keepdims=True))
    alpha = jnp.exp(m_prev - m_new)
    p = jnp.exp(s - m_new)
    l_sc[...] = alpha * l_sc[...] + jnp.sum(p, axis=-1, keepdims=True)
    pb = p.astype(BF16)
    pv_heads = []
    for h in range(N_HEADS):
        pv_h = jnp.zeros((rows, V_DIM), F32)
        for g in range(group):
            v_h = v_pages[g][pl.ds(h, PAGE_SIZE, stride=N_HEADS), :].astype(BF16)
            pv_h = pv_h + _dot(pb[:, g * PAGE_SIZE:(g + 1) * PAGE_SIZE], v_h)
        pv_heads.append(pv_h)
    pv = jnp.concatenate(pv_heads, axis=1)
    acc_sc[...] = alpha * acc_sc[...] + pv
    m_sc[...] = m_new

    @pl.when(j == pl.num_programs(1) - 1)
    def _():
        lam = _diff_lambda(lq_ref[...], lam_init)
        s_new = jnp.sum(qbd * kn_ref[pl.ds(b, 1), :], axis=-1, keepdims=True)
        m_fin = jnp.maximum(m_sc[...], s_new)
        a_fin = jnp.exp(m_sc[...] - m_fin)
        p_new = jnp.exp(s_new - m_fin)
        l_fin = a_fin * l_sc[...] + p_new
        on = (a_fin * acc_sc[...] + p_new * vn_ref[pl.ds(b, 1), :]) / l_fin
        coef = jnp.where(c // V_DIM == r // 2, jnp.where(r % 2 == 0, 1.0, -lam), 0.0)
        o = jnp.sum(on * coef, axis=0, keepdims=True)
        head = lax.broadcasted_iota(jnp.int32, (1, ATTN_WIDTH), 1) // V_DIM
        msq = jnp.zeros((1, ATTN_WIDTH), F32)
        for h in range(N_HEADS):
            ms_h = jnp.sum(jnp.where(head == h, o * o, 0.0), axis=-1, keepdims=True) * (1.0 / V_DIM)
            msq = jnp.where(head == h, ms_h, msq)
        o_ref[pl.ds(b, 1), :] = o * lax.rsqrt(msq + RMS_EPS) * gs_ref[...] * (1.0 - lam_init)


def _attn_decode(q, k_new, v_new, cache_k4, cache_v4, layer, page_table, lq, gs4, lam_init, group=8):
    nb, n_pages = page_table.shape
    const = lambda b, j, pt: (0, 0)

    def page_spec(g):
        return pl.BlockSpec((None, None, QK_COLS, PAGE_SIZE),
                            lambda b, j, pt: (layer, pt[b, j * group + g], 0, 0))

    full = lambda shape: pl.BlockSpec(shape, const)
    grid_spec = pltpu.PrefetchScalarGridSpec(
        num_scalar_prefetch=1,
        grid=(nb, n_pages // group),
        in_specs=[full((4, HEAD_DIM)), full((1, ATTN_WIDTH)), full((nb, QK_COLS)),
                  full((nb, QK_COLS)), full((nb, ATTN_WIDTH))]
                 + [page_spec(g) for g in range(group)] * 2,
        out_specs=full((nb, ATTN_WIDTH)),
        scratch_shapes=[pltpu.VMEM((2 * N_HEADS, 1), F32), pltpu.VMEM((2 * N_HEADS, 1), F32),
                        pltpu.VMEM((2 * N_HEADS, ATTN_WIDTH), F32)],
    )
    return pl.pallas_call(
        functools.partial(_attn_decode_kernel, lam_init=lam_init, group=group),
        grid_spec=grid_spec,
        out_shape=jax.ShapeDtypeStruct((nb, ATTN_WIDTH), F32),
        compiler_params=_params("arbitrary", "arbitrary"),
        name="attn_decode",
    )(page_table, lq, gs4, q, k_new, v_new, *([cache_k4] * group), *([cache_v4] * group))


def _conv_tail(c, lg, lb):
    mu = jnp.mean(c, axis=-1, keepdims=True)
    xc = c - mu
    var = jnp.mean(xc * xc, axis=-1, keepdims=True)
    return _silu(xc * lax.rsqrt(var + LN_EPS) * lg + lb)


def _out_proj(o, c, h, wo_ref):
    return (h + _dot(o.astype(BF16), wo_ref[0:ATTN_WIDTH, :])
            + _dot(c.astype(BF16), wo_ref[ATTN_WIDTH:D_MODEL, :]))


def _post_prompt_kernel(o_ref, glu_ref, halo_ref, h_ref, cw_ref, cb_ref, lg_ref, lb_ref, wo_ref,
                        h1_ref, x_sc, xs_sc, c_sc, *, tm, tiles_per_seq, chunk):
    first = pl.program_id(0) % tiles_per_seq == 0
    x_sc[0:HALO_ROWS, :] = jnp.where(first, 0.0, halo_ref[...])
    x_sc[HALO_ROWS:HALO_ROWS + tm, :] = glu_ref[...]
    x_sc[HALO_ROWS + tm:, :] = jnp.zeros((SUBLANES, CONV_CH), F32)
    for s in range(SUBLANES):
        xs_sc[s] = x_sc[s:s + tm + HALO_ROWS, :]
    shift = HALO_ROWS - CONV_PAD

    def chunk_body(t, carry):
        r0 = pl.multiple_of(t * chunk, chunk)
        acc = jnp.broadcast_to(cb_ref[...], (chunk, CONV_CH))
        for w in range(CONV_WIDTH):
            a, s = divmod(w + shift, SUBLANES)
            acc = acc + xs_sc[s, pl.ds(r0 + a * SUBLANES, chunk), :] * cw_ref[w:w + 1, :]
        c_sc[pl.ds(r0, chunk), :] = _conv_tail(acc, lg_ref[...], lb_ref[...])
        return carry

    lax.fori_loop(0, tm // chunk, chunk_body, 0)
    h1_ref[...] = _out_proj(o_ref[...], c_sc[...], h_ref[...], wo_ref)


def _post_prompt(o, glu, h, cw, cb, lg, lb, wo_b, seq, tm=256, chunk=32):
    m = h.shape[0]
    row = lambda i: (i, 0)
    halo_blocks = tm // HALO_ROWS
    return pl.pallas_call(
        functools.partial(_post_prompt_kernel, tm=tm, tiles_per_seq=seq // tm, chunk=chunk),
        grid=(m // tm,),
        in_specs=[pl.BlockSpec((tm, ATTN_WIDTH), row), pl.BlockSpec((tm, CONV_CH), row),
                  pl.BlockSpec((HALO_ROWS, CONV_CH),
                               lambda i: (jnp.maximum(i * halo_blocks - 1, 0), 0)),
                  pl.BlockSpec((tm, D_MODEL), row),
                  _resident((HALO_ROWS, CONV_CH)), _resident((1, CONV_CH)),
                  _resident((1, CONV_CH)), _resident((1, CONV_CH)),
                  _resident((D_MODEL, D_MODEL))],
        out_specs=pl.BlockSpec((tm, D_MODEL), row),
        out_shape=jax.ShapeDtypeStruct((m, D_MODEL), F32),
        scratch_shapes=[pltpu.VMEM((tm + HALO_ROWS + SUBLANES, CONV_CH), F32),
                        pltpu.VMEM((SUBLANES, tm + HALO_ROWS, CONV_CH), F32),
                        pltpu.VMEM((tm, CONV_CH), F32)],
        compiler_params=_params("parallel"),
        name="post_prompt",
    )(o, glu, glu, h, cw, cb, lg, lb, wo_b)


def _post_sample_kernel(o_ref, glu_ref, st_ref, h_ref, cw_ref, cb_ref, lg_ref, lb_ref, wo_ref, h1_ref):
    glu = glu_ref[...]
    acc = cb_ref[...] + glu * cw_ref[CONV_PAD:CONV_WIDTH, :]
    for w in range(CONV_PAD):
        acc = acc + st_ref[w] * cw_ref[w:w + 1, :]
    c = _conv_tail(acc, lg_ref[...], lb_ref[...])
    h1_ref[...] = _out_proj(o_ref[...], c, h_ref[...], wo_ref)


def _post_sample(o, glu, state2d, h, cw, cb, lg, lb, wo_b):
    m = h.shape[0]
    args = (o, glu, state2d, h, cw, cb, lg, lb, wo_b)
    return pl.pallas_call(
        _post_sample_kernel,
        grid=(1,),
        in_specs=[_resident(a.shape) for a in args],
        out_specs=pl.BlockSpec((m, D_MODEL), lambda i: (0, 0)),
        out_shape=jax.ShapeDtypeStruct((m, D_MODEL), F32),
        compiler_params=_params("arbitrary"),
        name="post_sample",
    )(*args)


def _ffn_dense_kernel(h_ref, g_ref, wgu_ref, wd_ref, out_ref, *, n_chunks):
    x = h_ref[...]
    b = _rms(x, g_ref[...]).astype(BF16)
    acc = x
    fc = DENSE_FF // n_chunks
    for ci in range(n_chunks):
        c0 = ci * fc
        gate = _dot(b, wgu_ref[:, c0:c0 + fc])
        up = _dot(b, wgu_ref[:, DENSE_FF + c0:DENSE_FF + c0 + fc])
        acc = acc + _dot((_silu(gate) * up).astype(BF16), wd_ref[c0:c0 + fc, :])
    out_ref[...] = acc


def _ffn_dense(h, g, wgu_b, wd_b, tm):
    m = h.shape[0]
    row = lambda i: (i, 0)
    return pl.pallas_call(
        functools.partial(_ffn_dense_kernel, n_chunks=2),
        grid=(m // tm,),
        in_specs=[pl.BlockSpec((tm, D_MODEL), row), _resident((1, D_MODEL)),
                  _resident((D_MODEL, 2 * DENSE_FF)), _resident((DENSE_FF, D_MODEL))],
        out_specs=pl.BlockSpec((tm, D_MODEL), row),
        out_shape=jax.ShapeDtypeStruct((m, D_MODEL), F32),
        compiler_params=_params("parallel"),
        name="ffn_dense",
    )(h, g, wgu_b, wd_b)


def _moe_kernel(h_ref, g_ref, wr_ref, br_ref, wgu_ref, wd_ref, out_ref, b_sc, gates_sc, acc_sc):
    e = pl.program_id(1)
    lane = lax.broadcasted_iota(jnp.int32, gates_sc.shape, 1)

    @pl.when(e == 0)
    def _():
        b = _rms(h_ref[...], g_ref[...])
        b_sc[...] = b.astype(BF16)
        logits = jnp.dot(b, wr_ref[...], preferred_element_type=F32,
                         precision=lax.Precision.HIGHEST) + br_ref[...]
        m1 = jnp.max(logits, axis=-1, keepdims=True)
        i1 = jnp.min(jnp.where(logits == m1, lane, LANES), axis=-1, keepdims=True)
        rest = jnp.where(lane == i1, -jnp.inf, logits)
        m2 = jnp.max(rest, axis=-1, keepdims=True)
        i2 = jnp.min(jnp.where(rest == m2, lane, LANES), axis=-1, keepdims=True)
        t = jnp.exp(m2 - m1)
        w1 = 1.0 / (1.0 + t)
        gates_sc[...] = jnp.where(lane == i1, w1, 0.0) + jnp.where(lane == i2, t * w1, 0.0)
        acc_sc[...] = jnp.zeros(acc_sc.shape, F32)

    b = b_sc[...]
    gate = _dot(b, wgu_ref[:, 0:EXPERT_FF])
    up = _dot(b, wgu_ref[:, EXPERT_FF:2 * EXPERT_FF])
    y = _dot((_silu(gate) * up).astype(BF16), wd_ref[...])
    g_e = jnp.sum(jnp.where(lane == e, gates_sc[...], 0.0), axis=-1, keepdims=True)
    acc_sc[...] += g_e * y

    @pl.when(e == N_EXPERTS - 1)
    def _():
        out_ref[...] = h_ref[...] + acc_sc[...]


def _moe(h, g, wr_pad, br_pad, wgu_b, wd_b, tm):
    m = h.shape[0]
    row = lambda i, e: (i, 0)
    const = lambda i, e: (0, 0)
    return pl.pallas_call(
        _moe_kernel,
        grid=(m // tm, N_EXPERTS),
        in_specs=[pl.BlockSpec((tm, D_MODEL), row), pl.BlockSpec((1, D_MODEL), const),
                  pl.BlockSpec((D_MODEL, LANES), const), pl.BlockSpec((1, LANES), const),
                  pl.BlockSpec((None, D_MODEL, 2 * EXPERT_FF), lambda i, e: (e, 0, 0)),
                  pl.BlockSpec((None, EXPERT_FF, D_MODEL), lambda i, e: (e, 0, 0))],
        out_specs=pl.BlockSpec((tm, D_MODEL), row),
        out_shape=jax.ShapeDtypeStruct((m, D_MODEL), F32),
        scratch_shapes=[pltpu.VMEM((tm, D_MODEL), BF16), pltpu.VMEM((tm, LANES), F32),
                        pltpu.VMEM((tm, D_MODEL), F32)],
        compiler_params=_params("parallel", "arbitrary"),
        name="moe",
    )(h, g, wr_pad, br_pad, wgu_b, wd_b)


def _ple_kernel(h_ref, p_ref, gg_ref, wg_ref, wp_ref, go_ref, out_ref):
    x = h_ref[...]
    gate = jax.nn.sigmoid(_dot(_rms(x, gg_ref[...]).astype(BF16), wg_ref[...]))
    t = _dot(p_ref[...].astype(BF16), wp_ref[...])
    out_ref[...] = x + _rms(t, go_ref[...]) * gate


def _ple(h, p, gg, wg_b, wp_b, go, tm):
    m = h.shape[0]
    row = lambda i: (i, 0)
    return pl.pallas_call(
        _ple_kernel,
        grid=(m // tm,),
        in_specs=[pl.BlockSpec((tm, D_MODEL), row), pl.BlockSpec((tm, D_PLE), row),
                  _resident((1, D_MODEL)), _resident((D_MODEL, D_MODEL)),
                  _resident((D_PLE, D_MODEL)), _resident((1, D_MODEL))],
        out_specs=pl.BlockSpec((tm, D_MODEL), row),
        out_shape=jax.ShapeDtypeStruct((m, D_MODEL), F32),
        compiler_params=_params("parallel"),
        name="ple",
    )(h, p, gg, wg_b, wp_b, go)


def kernel(x_prompt, x_sample, cache_k, cache_v, state_conv, page_table, p_prompt, p_sample, g_mix, w_in, q_norm, k_norm, lambda_qk, g_subln, conv_w, conv_b, conv_ln_g, conv_ln_b, w_out, g_ffn, w_dense_gu, w_dense_down, w_router, b_router, w_exp_gu, w_exp_down, g_ple_gate, w_ple_gate, w_ple, g_ple_out):
    depth = w_in.shape[0]
    batch, seq, _ = x_prompt.shape
    dec_batch = x_sample.shape[0]
    mp, ms = batch * seq, dec_batch
    n_pool = cache_k.shape[1]

    cache_k4 = jnp.transpose(cache_k, (0, 1, 3, 4, 5, 2)).reshape(depth, n_pool, QK_COLS, PAGE_SIZE)
    cache_v4 = cache_v.reshape(depth, n_pool, PAGE_SIZE * N_HEADS, V_DIM)
    seg = (jnp.arange(QK_COLS)[:, None] // HEAD_DIM == jnp.arange(QK_COLS)[None, :] // HEAD_DIM).astype(BF16)
    w_in_b, w_out_b = w_in.astype(BF16), w_out.astype(BF16)
    w_dgu_b, w_dd_b = w_dense_gu.astype(BF16), w_dense_down.astype(BF16)
    w_egu_b, w_ed_b = w_exp_gu.astype(BF16), w_exp_down.astype(BF16)
    w_pg_b, w_p_b = w_ple_gate.astype(BF16), w_ple.astype(BF16)
    wr_pad = jnp.pad(w_router, ((0, 0), (0, 0), (0, LANES - N_EXPERTS)))
    br_pad = jnp.pad(b_router, ((0, 0), (0, LANES - N_EXPERTS)), constant_values=-1e30)
    cw_pad = jnp.pad(conv_w, ((0, 0), (0, HALO_ROWS - CONV_WIDTH), (0, 0)))
    row = lambda a: a.reshape(1, -1)

    hp = x_prompt.reshape(mp, D_MODEL)
    hs = x_sample.reshape(ms, D_MODEL)
    outs = [[] for _ in range(6)]
    for i in range(depth):
        lam0 = _lambda_init(i)
        qn = row(jnp.tile(q_norm[i], 2 * N_HEADS))
        kn = row(jnp.tile(k_norm[i], 2 * N_HEADS))
        gs = row(g_subln[i])
        gs4 = row(jnp.tile(g_subln[i], N_HEADS))
        conv_args = (cw_pad[i], row(conv_b[i]), row(conv_ln_g[i]), row(conv_ln_b[i]), w_out_b[i])

        def ffn_ple(h, p, tm):
            if i % 2 == 0:
                h = _ffn_dense(h, row(g_ffn[i]), w_dgu_b[i // 2], w_dd_b[i // 2], tm)
            else:
                j = i // 2
                h = _moe(h, row(g_ffn[i]), wr_pad[j], row(br_pad[j]), w_egu_b[j], w_ed_b[j], tm)
            return _ple(h, p, row(g_ple_gate[i]), w_pg_b[i], w_p_b[i], row(g_ple_out[i]), tm)

        q, k, v, glu = _inproj(hp, row(g_mix[i]), w_in_b[i], qn, kn, seg, 512)
        o = _attn_prompt(q, k, v, lambda_qk[i], gs, lam0, batch, seq)
        hp = _post_prompt(o, glu, hp, *conv_args, seq)
        hp = ffn_ple(hp, p_prompt[i].reshape(mp, D_PLE), 512)
        outs[0].append(k.reshape(batch, seq, N_HEADS, 2, HEAD_DIM))
        outs[1].append(v.reshape(batch, seq, N_HEADS, V_DIM))
        outs[2].append(glu.reshape(batch, seq, CONV_CH)[:, seq - CONV_PAD:])

        q, k, v, glu = _inproj(hs, row(g_mix[i]), w_in_b[i], qn, kn, seg, ms)
        o = _attn_decode(q, k, v, cache_k4, cache_v4, i, page_table, lambda_qk[i], gs4, lam0)
        hs = _post_sample(o, glu, jnp.transpose(state_conv[i], (1, 0, 2)), hs, *conv_args)
        hs = ffn_ple(hs, p_sample[i].reshape(ms, D_PLE), ms)
        outs[3].append(k.reshape(ms, 1, N_HEADS, 2, HEAD_DIM))
        outs[4].append(v.reshape(ms, 1, N_HEADS, V_DIM))
        outs[5].append(jnp.concatenate([state_conv[i][:, 1:], glu[:, None, :]], axis=1))

    return (hp.reshape(batch, seq, D_MODEL), hs.reshape(ms, 1, D_MODEL),
            *[jnp.stack(o) for o in outs])
```
